```python
import jax, jax.numpy as jnp
from jax import lax
import numpy as np

D_MODEL = 2048
BATCH = 2
SEQ = 8192
DEPTH = 2

HEAD_DIM = 128
ROPE_DIM = HEAD_DIM // 4
ROPE_THETA = 500000.0
Q_BLOCK = 128
N_HEADS_A = 8
KV_RANK = 256
NOPE_DIM = HEAD_DIM - ROPE_DIM
N_IDX_HEADS = 16
IDX_DIM = 64
IDX_ROPE_DIM = IDX_DIM // 4
TOPK_MAX = 256
N_HEADS_B = 8
N_HEADS_C = 8
WIDTH_A = N_HEADS_A * HEAD_DIM
WIDTH_B = N_HEADS_B * HEAD_DIM
WIDTH_C = N_HEADS_C * HEAD_DIM
N_BRANCH = 3
N_EXPERTS = 32
TOP_K_EXPERTS = 4
D_EXPERT = 1024
SWIGLU_LIMIT = 7.0
SWIGLU_ALPHA = 1.702
PLE_DIM = 256
LN_EPS = 1e-5
RMS_EPS = 1e-6
DEEPNORM_ALPHA = (2 * DEPTH) ** 0.25
DEEPNORM_BETA = (8 * DEPTH) ** -0.25

IN_SPLITS = (WIDTH_A, KV_RANK, ROPE_DIM, N_IDX_HEADS * IDX_DIM, IDX_DIM, N_IDX_HEADS,
             WIDTH_B, WIDTH_B, WIDTH_B, N_HEADS_B,
             WIDTH_C, WIDTH_C, WIDTH_C,
             N_BRANCH * D_MODEL)
IN_WIDTH = sum(IN_SPLITS)

kernel_name = 'hybrid_dsa_fox_stickbreak_moe_deepnorm'


def split_columns(h):
    parts = []
    start = 0
    for width in IN_SPLITS:
        parts.append(h[..., start:start + width])
        start += width
    return parts


def layer_norm(x, g, b):
    xf = x.astype(jnp.float32)
    mu = jnp.mean(xf, axis=-1, keepdims=True)
    var = jnp.mean(jnp.square(xf - mu), axis=-1, keepdims=True)
    y = (xf - mu) * lax.rsqrt(var + LN_EPS)
    return (y * g.astype(jnp.float32) + b.astype(jnp.float32)).astype(x.dtype)


def rms_norm(x, g):
    xf = x.astype(jnp.float32)
    y = xf * lax.rsqrt(jnp.mean(jnp.square(xf), axis=-1, keepdims=True) + RMS_EPS)
    return (y * g.astype(jnp.float32)).astype(x.dtype)


def rope_tables(seq_len, rot_dim):
    inv_freq = ROPE_THETA ** (-jnp.arange(0, rot_dim, 2, dtype=jnp.float32) / rot_dim)
    ang = jnp.arange(seq_len, dtype=jnp.float32)[:, None] * inv_freq[None, :]
    return jnp.cos(ang), jnp.sin(ang)


def apply_rope(x, cos, sin):
    shape = (1, x.shape[1]) + (1,) * (x.ndim - 3) + (cos.shape[-1],)
    c = cos.reshape(shape)
    s = sin.reshape(shape)
    x1, x2 = jnp.split(x.astype(jnp.float32), 2, axis=-1)
    return jnp.concatenate([x1 * c - x2 * s, x2 * c + x1 * s], axis=-1).astype(x.dtype)


def to_blocks(a):
    b, s = a.shape[:2]
    a = a.reshape((b, s // Q_BLOCK, Q_BLOCK) + a.shape[2:])
    return jnp.moveaxis(a, 1, 0)


def from_blocks(a):
    a = jnp.moveaxis(a, 0, 1)
    return a.reshape((a.shape[0], a.shape[1] * a.shape[2]) + a.shape[3:])


def split_heads(t, n_heads):
    return t.reshape(t.shape[0], t.shape[1], n_heads, HEAD_DIM)


def dsa_attention(q_a, ckv, k_rope, q_idx, k_idx, w_idx, w_uk, w_uv, cos_r, sin_r, cos_i, sin_i):
    B, S = q_a.shape[:2]
    topk = min(TOPK_MAX, S // 4)
    q = q_a.reshape(B, S, N_HEADS_A, HEAD_DIM)
    q_rope = apply_rope(q[..., :ROPE_DIM], cos_r, sin_r)
    q_lat = jnp.einsum('bshn,rhn->bshr', q[..., ROPE_DIM:], w_uk)
    k_rope = apply_rope(k_rope, cos_r, sin_r)
    qi = q_idx.reshape(B, S, N_IDX_HEADS, IDX_DIM)
    qi = jnp.concatenate([apply_rope(qi[..., :IDX_ROPE_DIM], cos_i, sin_i), qi[..., IDX_ROPE_DIM:]], axis=-1)
    ki = jnp.concatenate([apply_rope(k_idx[..., :IDX_ROPE_DIM], cos_i, sin_i), k_idx[..., IDX_ROPE_DIM:]], axis=-1)
    wi = w_idx.astype(jnp.float32) * (N_IDX_HEADS ** -0.5 * IDX_DIM ** -0.5)
    key_pos = jnp.arange(S)
    b_idx = jnp.arange(B)[:, None, None]
    scale = HEAD_DIM ** -0.5

    def block(args):
        qlat_b, qr_b, qi_b, wi_b, pos_b = args
        causal = key_pos[None, :] <= pos_b[:, None]
        dots = jnp.einsum('bthd,bsd->bths', qi_b, ki).astype(jnp.float32)
        score = jnp.einsum('bth,bths->bts', wi_b, jax.nn.relu(dots))
        score = jnp.where(causal[None], score, -jnp.inf)
        _, sel = lax.top_k(score, topk)
        valid = sel <= pos_b[None, :, None]
        c_sel = ckv[b_idx, sel]
        kr_sel = k_rope[b_idx, sel]
        logits = (jnp.einsum('bthr,btkr->bhtk', qlat_b, c_sel)
                  + jnp.einsum('bthe,btke->bhtk', qr_b, kr_sel)).astype(jnp.float32) * scale
        logits = jnp.where(valid[:, None], logits, -jnp.inf)
        probs = jax.nn.softmax(logits, axis=-1).astype(c_sel.dtype)
        o_lat = jnp.einsum('bhtk,btkr->bthr', probs, c_sel)
        return jnp.einsum('bthr,rhv->bthv', o_lat, w_uv)

    pos = key_pos.reshape(-1, Q_BLOCK)
    out = lax.map(block, (to_blocks(q_lat), to_blocks(q_rope), to_blocks(qi), to_blocks(wi), pos))
    return from_blocks(out).reshape(B, S, WIDTH_A)


def forgetting_attention(q, k, v, f_logit, b_forget):
    B, S = q.shape[:2]
    q, k, v = split_heads(q, N_HEADS_B), split_heads(k, N_HEADS_B), split_heads(v, N_HEADS_B)
    log_f = jax.nn.log_sigmoid(f_logit.astype(jnp.float32) + b_forget.astype(jnp.float32))
    cum = lax.cumsum(log_f, axis=1)
    cum_k = jnp.transpose(cum, (0, 2, 1))
    key_pos = jnp.arange(S)
    scale = HEAD_DIM ** -0.5

    def block(args):
        q_b, cum_b, pos_b = args
        causal = key_pos[None, :] <= pos_b[:, None]
        logits = jnp.einsum('bthd,bshd->bhts', q_b, k).astype(jnp.float32) * scale
        logits = logits + jnp.transpose(cum_b, (0, 2, 1))[..., None] - cum_k[:, :, None, :]
        logits = jnp.where(causal, logits, -jnp.inf)
        probs = jax.nn.softmax(logits, axis=-1).astype(v.dtype)
        return jnp.einsum('bhts,bshd->bthd', probs, v)

    pos = key_pos.reshape(-1, Q_BLOCK)
    out = lax.map(block, (to_blocks(q), to_blocks(cum), pos))
    return from_blocks(out).reshape(B, S, WIDTH_B)


def stick_breaking_attention(q, k, v):
    B, S = q.shape[:2]
    q, k, v = split_heads(q, N_HEADS_C), split_heads(k, N_HEADS_C), split_heads(v, N_HEADS_C)
    key_pos = jnp.arange(S)
    scale = HEAD_DIM ** -0.5

    def block(args):
        q_b, pos_b = args
        strict = key_pos[None, :] < pos_b[:, None]
        z = jnp.einsum('bthd,bshd->bhts', q_b, k).astype(jnp.float32) * scale
        log_keep = jnp.where(strict, jax.nn.log_sigmoid(-z), 0.0)
        later = lax.cumsum(log_keep, axis=3, reverse=True) - log_keep
        weights = jnp.where(strict, jnp.exp(jax.nn.log_sigmoid(z) + later), 0.0)
        return jnp.einsum('bhts,bshd->bthd', weights.astype(v.dtype), v)

    pos = key_pos.reshape(-1, Q_BLOCK)
    out = lax.map(block, (to_blocks(q), pos))
    return from_blocks(out).reshape(B, S, WIDTH_C)


def hybrid_mixer(x, w_in, b_forget, g_ckv, w_uk, w_uv, w_br_a, w_br_b, w_br_c, w_out, ropes):
    B, S, D = x.shape
    cos_r, sin_r, cos_i, sin_i = ropes
    h = jnp.einsum('bsd,dc->bsc', x, w_in)
    (q_a, ckv, k_rope, q_idx, k_idx, w_idx,
     q_b, k_b, v_b, f_b, q_c, k_c, v_c, gate_logits) = split_columns(h)
    ckv = rms_norm(ckv, g_ckv)
    o_a = dsa_attention(q_a, ckv, k_rope, q_idx, k_idx, w_idx, w_uk, w_uv, cos_r, sin_r, cos_i, sin_i)
    o_b = forgetting_attention(q_b, k_b, v_b, f_b, b_forget)
    o_c = stick_breaking_attention(q_c, k_c, v_c)
    gates = jax.nn.sigmoid(gate_logits.astype(jnp.float32)).astype(x.dtype).reshape(B, S, N_BRANCH, D)
    merged = (gates[:, :, 0] * (o_a @ w_br_a)
              + gates[:, :, 1] * (o_b @ w_br_b)
              + gates[:, :, 2] * (o_c @ w_br_c))
    return merged @ w_out


def moe_ffn(x, w_router, b_router, w_exp_gate, b_exp_gate, w_exp_up, b_exp_up, w_exp_down, b_exp_down):
    B, S, D = x.shape
    xf = x.reshape(B * S, D)
    logits = (xf @ w_router).astype(jnp.float32) + b_router.astype(jnp.float32)
    top_val, top_idx = lax.top_k(logits, TOP_K_EXPERTS)
    top_w = jax.nn.softmax(top_val, axis=-1)
    combine = jnp.sum(jax.nn.one_hot(top_idx, N_EXPERTS, dtype=jnp.float32) * top_w[..., None], axis=1)

    def expert(acc, params):
        wg, bg, wu, bu, wd, bd, cw = params
        g = jnp.minimum(xf @ wg + bg, SWIGLU_LIMIT)
        u = jnp.clip(xf @ wu + bu, -SWIGLU_LIMIT, SWIGLU_LIMIT)
        hid = (u + 1.0) * (g * jax.nn.sigmoid(SWIGLU_ALPHA * g))
        return acc + cw[:, None].astype(xf.dtype) * (hid @ wd + bd), None

    y, _ = lax.scan(expert, jnp.zeros_like(xf),
                    (w_exp_gate, b_exp_gate, w_exp_up, b_exp_up, w_exp_down, b_exp_down, combine.T))
    return y.reshape(B, S, D)


def setup_inputs(seed: int = 0) -> dict:
    key = jax.random.key(seed)
    ks = jax.random.split(key, 32)
    f32 = jnp.float32

    def nrm(k, shape, scale):
        return jax.random.normal(k, shape, f32) * scale

    L, D, E, F = DEPTH, D_MODEL, N_EXPERTS, D_EXPERT
    return {
        'x': nrm(ks[0], (BATCH, SEQ, D), 1.0),
        'p': nrm(ks[1], (DEPTH, BATCH, SEQ, PLE_DIM), 1.0),
        'w_in': nrm(ks[2], (L, D, IN_WIDTH), D ** -0.5),
        'b_forget': jax.random.uniform(ks[3], (L, N_HEADS_B), f32, 1.0, 4.0),
        'g_ckv': 1.0 + nrm(ks[4], (L, KV_RANK), 0.02),
        'w_uk': nrm(ks[5], (L, KV_RANK, N_HEADS_A, NOPE_DIM), KV_RANK ** -0.5),
        'w_uv': nrm(ks[6], (L, KV_RANK, N_HEADS_A, HEAD_DIM), KV_RANK ** -0.5),
        'w_br_a': nrm(ks[7], (L, WIDTH_A, D), WIDTH_A ** -0.5),
        'w_br_b': nrm(ks[8], (L, WIDTH_B, D), WIDTH_B ** -0.5),
        'w_br_c': nrm(ks[9], (L, WIDTH_C, D), WIDTH_C ** -0.5),
        'w_out': nrm(ks[10], (L, D, D), D ** -0.5 * DEEPNORM_BETA),
        'ln1_g': 1.0 + nrm(ks[11], (L, D), 0.02),
        'ln1_b': nrm(ks[12], (L, D), 0.02),
        'w_router': nrm(ks[13], (L, D, E), D ** -0.5),
        'b_router': nrm(ks[14], (L, E), 0.01),
        'w_exp_gate': nrm(ks[15], (L, E, D, F), D ** -0.5),
        'b_exp_gate': nrm(ks[16], (L, E, F), 0.01),
        'w_exp_up': nrm(ks[17], (L, E, D, F), D ** -0.5),
        'b_exp_up': nrm(ks[18], (L, E, F), 0.01),
        'w_exp_down': nrm(ks[19], (L, E, F, D), F ** -0.5 * DEEPNORM_BETA),
        'b_exp_down': nrm(ks[20], (L, E, D), 0.01),
        'w_ple_gate': nrm(ks[21], (L, D, D), D ** -0.5),
        'w_ple': nrm(ks[22], (L, PLE_DIM, D), PLE_DIM ** -0.5 * DEEPNORM_BETA),
        'ln2_g': 1.0 + nrm(ks[23], (L, D), 0.02),
        'ln2_b': nrm(ks[24], (L, D), 0.02),
    }


def reference(x, p, w_in, b_forget, g_ckv, w_uk, w_uv, w_br_a, w_br_b, w_br_c, w_out, ln1_g, ln1_b,
              w_router, b_router, w_exp_gate, b_exp_gate, w_exp_up, b_exp_up, w_exp_down, b_exp_down,
              w_ple_gate, w_ple, ln2_g, ln2_b):
    S = x.shape[1]
    cos_r, sin_r = rope_tables(S, ROPE_DIM)
    cos_i, sin_i = rope_tables(S, IDX_ROPE_DIM)
    ropes = (cos_r, sin_r, cos_i, sin_i)
    for i in range(DEPTH):
        mix = hybrid_mixer(x, w_in[i], b_forget[i], g_ckv[i], w_uk[i], w_uv[i],
                           w_br_a[i], w_br_b[i], w_br_c[i], w_out[i], ropes)
        x = layer_norm(DEEPNORM_ALPHA * x + mix, ln1_g[i], ln1_b[i])
        h = DEEPNORM_ALPHA * x + moe_ffn(x, w_router[i], b_router[i], w_exp_gate[i], b_exp_gate[i],
                                         w_exp_up[i], b_exp_up[i], w_exp_down[i], b_exp_down[i])
        ple = jax.nn.sigmoid(h @ w_ple_gate[i]) * (p[i] @ w_ple[i])
        x = layer_norm(h + ple, ln2_g[i], ln2_b[i])
    return x
```

```python
import functools

import jax
import jax.numpy as jnp
from jax import lax
from jax.experimental import pallas as pl
from jax.experimental.pallas import tpu as pltpu

HEAD_DIM = 128
ROPE_DIM = 32
NOPE_DIM = HEAD_DIM - ROPE_DIM
ROPE_THETA = 500000.0
N_HEADS_A = 8
KV_RANK = 256
N_IDX_HEADS = 16
IDX_DIM = 64
IDX_ROPE_DIM = 16
TOPK_MAX = 256
N_HEADS_B = 8
N_HEADS_C = 8
N_BRANCH = 3
TOP_K_EXPERTS = 4
SWIGLU_LIMIT = 7.0
SWIGLU_ALPHA = 1.702
LN_EPS = 1e-5
RMS_EPS = 1e-6

LANES = 128
QCAT = KV_RANK + LANES
INT_MIN = -(2 ** 31)
NEG_BIG = -1e30
VMEM_LIMIT = 56 * 1024 * 1024

F32 = jnp.float32
BF16 = jnp.bfloat16


def _cp(*sem, vmem=VMEM_LIMIT):
    return pltpu.CompilerParams(dimension_semantics=sem, vmem_limit_bytes=vmem)


def _dot(a, b):
    return jnp.dot(a, b, preferred_element_type=F32)


def _dot_nt(a, b):
    return lax.dot_general(a, b, (((1,), (1,)), ((), ())), preferred_element_type=F32)


def _split3(x):
    hi = x.astype(BF16)
    r1 = x - hi.astype(F32)
    mid = r1.astype(BF16)
    lo = (r1 - mid.astype(F32)).astype(BF16)
    return hi, mid, lo


def _layer_norm(h, g, b):
    mu = jnp.mean(h, axis=-1, keepdims=True)
    d = h - mu
    var = jnp.mean(d * d, axis=-1, keepdims=True)
    return d * lax.rsqrt(var + LN_EPS) * g + b


def _rope128(x, cos, sin, period, half):
    lane = lax.broadcasted_iota(jnp.int32, x.shape, 1)
    first = (lane % period) < half
    partner = jnp.where(first, -pltpu.roll(x, LANES - half, 1), pltpu.roll(x, half, 1))
    return x * cos + partner * sin


def _matmul_kernel(x_ref, w_ref, o_ref, *, act):
    y = _dot(x_ref[...], w_ref[...])
    if act == "sigmoid":
        y = jax.nn.sigmoid(y)
    o_ref[...] = y.astype(o_ref.dtype)


def _matmul(x, w, out_dtype, act=None, tm=1024, tn=512):
    m, k = x.shape
    n = w.shape[1]
    tm = min(tm, m)
    tn = next(t for t in (tn, 384, 256, LANES) if n % t == 0)
    assert m % tm == 0
    return pl.pallas_call(
        functools.partial(_matmul_kernel, act=act),
        grid=(m // tm, n // tn),
        in_specs=[pl.BlockSpec((tm, k), lambda i, j: (i, 0)),
                  pl.BlockSpec((k, tn), lambda i, j: (0, j))],
        out_specs=pl.BlockSpec((tm, tn), lambda i, j: (i, j)),
        out_shape=jax.ShapeDtypeStruct((m, n), out_dtype),
        compiler_params=_cp("parallel", "parallel"),
    )(x, w)


def _prep_q_kernel(qa_ref, qx_ref, cr_ref, sr_ref, ci_ref, si_ref, wc_ref, pm_ref,
                   qcat_ref, qi_ref):
    cr, sr, ci, si = cr_ref[...], sr_ref[...], ci_ref[...], si_ref[...]
    for h in range(N_HEADS_A):
        xa = qa_ref[:, h * HEAD_DIM:(h + 1) * HEAD_DIM]
        ra = _rope128(xa, cr, sr, HEAD_DIM, ROPE_DIM // 2)
        lhs = jnp.concatenate([xa.astype(BF16), ra.astype(BF16)], axis=1)
        qcat_ref[0, h] = _dot(lhs, wc_ref[h]).astype(BF16)
    groups = []
    for g in range(N_IDX_HEADS * IDX_DIM // LANES):
        xi = qx_ref[:, g * LANES:(g + 1) * LANES]
        groups.append(_rope128(xi, ci, si, IDX_DIM, IDX_ROPE_DIM // 2).astype(BF16))
    qi_ref[...] = _dot(jnp.concatenate(groups, axis=1), pm_ref[...]).astype(BF16)


def _prep_q(hf, tabs, wcomb, place, b, s, tm=256):
    n = b * s
    tm = min(tm, s)
    spt = s // tm
    wa = N_HEADS_A * HEAD_DIM
    wi = N_IDX_HEADS * IDX_DIM
    tab = pl.BlockSpec((tm, LANES), lambda i: (i % spt, 0))
    return pl.pallas_call(
        _prep_q_kernel,
        grid=(n // tm,),
        in_specs=[pl.BlockSpec((tm, wa), lambda i: (i, 0)),
                  pl.BlockSpec((tm, wi), lambda i: (i, wa // wi)),
                  tab, tab, tab, tab,
                  pl.BlockSpec(wcomb.shape, lambda i: (0, 0, 0)),
                  pl.BlockSpec(place.shape, lambda i: (0, 0))],
        out_specs=[pl.BlockSpec((1, N_HEADS_A, tm, QCAT), lambda i: (i // spt, 0, i % spt, 0)),
                   pl.BlockSpec((tm, N_IDX_HEADS * LANES), lambda i: (i, 0))],
        out_shape=[jax.ShapeDtypeStruct((b, N_HEADS_A, s, QCAT), BF16),
                   jax.ShapeDtypeStruct((n, N_IDX_HEADS * LANES), BF16)],
        compiler_params=_cp("parallel"),
    )(hf, hf, tabs["cos_r"], tabs["sin_r"], tabs["cos_i"], tabs["sin_i"], wcomb, place)


def _prep_k_kernel(ckv_ref, kr_ref, kx_ref, wx_ref, fb_ref, g_ref, bf_ref,
                   cr_ref, sr_ref, ci_ref, si_ref,
                   kcat_ref, ki_ref, wi_ref, cum_ref, carry_ref, *, tiles_per_seq, w_scale):
    i = pl.program_id(0)
    c = ckv_ref[...]
    c = c * lax.rsqrt(jnp.mean(c * c, axis=-1, keepdims=True) + RMS_EPS) * g_ref[...]
    kcat_ref[:, :KV_RANK] = c.astype(BF16)
    kcat_ref[:, KV_RANK:] = _rope128(kr_ref[...], cr_ref[...], sr_ref[...],
                                     HEAD_DIM, ROPE_DIM // 2).astype(BF16)
    ki_ref[...] = _rope128(kx_ref[...], ci_ref[...], si_ref[...],
                           IDX_DIM, IDX_ROPE_DIM // 2).astype(BF16)
    wi_ref[...] = wx_ref[...] * w_scale

    f = fb_ref[...] + bf_ref[...]
    logf = jnp.minimum(f, 0.0) - jnp.log1p(jnp.exp(-jnp.abs(f)))
    tm = logf.shape[0]
    tri = (lax.broadcasted_iota(jnp.int32, (tm, tm), 0)
           >= lax.broadcasted_iota(jnp.int32, (tm, tm), 1)).astype(BF16)
    hi, mid, lo = _split3(logf)
    local = _dot(tri, hi) + _dot(tri, mid) + _dot(tri, lo)

    @pl.when(i % tiles_per_seq == 0)
    def _():
        carry_ref[...] = jnp.zeros_like(carry_ref)

    cum = local + carry_ref[...]
    cum_ref[...] = cum
    carry_ref[...] = cum[tm - 1:tm, :]


def _prep_k(hf, tabs, g_ckv, b_forget, b, s, tm=512):
    n = b * s
    tm = min(tm, s)
    spt = s // tm
    base = (N_HEADS_A * HEAD_DIM + N_IDX_HEADS * IDX_DIM)
    small = (base + KV_RANK) // LANES
    tab = pl.BlockSpec((tm, LANES), lambda i: (i % spt, 0))

    def col(j):
        return pl.BlockSpec((tm, LANES), lambda i: (i, small + j))

    g = g_ckv.reshape(1, KV_RANK).astype(F32)
    bf = jnp.zeros((1, LANES), F32).at[0, :N_HEADS_B].set(b_forget.astype(F32))
    vec = lambda w: pl.BlockSpec((1, w), lambda i: (0, 0))
    return pl.pallas_call(
        functools.partial(_prep_k_kernel, tiles_per_seq=spt,
                          w_scale=N_IDX_HEADS ** -0.5 * IDX_DIM ** -0.5),
        grid=(n // tm,),
        in_specs=[pl.BlockSpec((tm, KV_RANK), lambda i: (i, base // KV_RANK)),
                  col(0), col(1), col(2), col(3), vec(KV_RANK), vec(LANES),
                  tab, tab, tab, tab],
        out_specs=[pl.BlockSpec((tm, QCAT), lambda i: (i, 0)),
                   pl.BlockSpec((tm, LANES), lambda i: (i, 0)),
                   pl.BlockSpec((tm, LANES), lambda i: (i, 0)),
                   pl.BlockSpec((tm, LANES), lambda i: (i, 0))],
        out_shape=[jax.ShapeDtypeStruct((n, QCAT), BF16),
                   jax.ShapeDtypeStruct((n, LANES), BF16),
                   jax.ShapeDtypeStruct((n, LANES), F32),
                   jax.ShapeDtypeStruct((n, LANES), F32)],
        scratch_shapes=[pltpu.VMEM((1, LANES), F32)],
        compiler_params=_cp("arbitrary"),
    )(hf, hf, hf, hf, hf, g, bf, tabs["cos_r"], tabs["sin_r"], tabs["cos_ik"], tabs["sin_ik"])


def _sortable(x):
    bits = pltpu.bitcast(x, jnp.int32)
    return bits ^ ((bits >> 31) & jnp.int32(0x7FFFFFFF))


def _dsa_kernel(qi_ref, wi_ref, qcat_ref, ki_ref, kcat_ref, wuv_ref, o_ref,
                sc_ref, wb_ref, t_ref, j_ref, m_ref, l_ref, acc_ref, *, tq, rg, topk, idx_bits, scale):
    i = pl.program_id(1)
    nch = i + 1
    n_heads = N_HEADS_A
    sub = lax.broadcasted_iota(jnp.int32, (tq, tq), 0)
    lane = lax.broadcasted_iota(jnp.int32, (tq, tq), 1)
    reps = tq // LANES

    for h in range(N_IDX_HEADS):
        wb_ref[h] = jnp.broadcast_to(wi_ref[0, :, h:h + 1], (tq, LANES))

    def score_chunk(c, carry):
        kc = ki_ref[0, pl.ds(pl.multiple_of(c * tq, tq), tq), :]
        acc = jnp.zeros((tq, tq), F32)
        for h in range(N_IDX_HEADS):
            d = _dot_nt(qi_ref[0, :, h * LANES:(h + 1) * LANES], kc)
            w = wb_ref[h]
            acc = acc + jnp.concatenate([w] * reps, axis=1) * jnp.maximum(d, 0.0)
        key = _sortable(acc)
        key = jnp.where(lane + (c - i) * tq <= sub, key, INT_MIN)
        sc_ref[c] = key
        return carry

    lax.fori_loop(0, nch, score_chunk, 0)

    def select_rows(g, carry):
        r0 = pl.multiple_of(g * rg, rg)

        def count(pred):
            def body(c, cnt):
                x = jnp.where(pred(sc_ref[c, pl.ds(r0, rg), :], c), 1, 0)
                for r in range(reps):
                    cnt = cnt + x[:, r * LANES:(r + 1) * LANES]
                return cnt
            cnt = lax.fori_loop(0, nch, body, jnp.zeros((rg, LANES), jnp.int32))
            return jnp.sum(cnt, axis=1, keepdims=True)

        def bit_body(b, thr):
            cand = thr + lax.shift_left(jnp.int32(1), 31 - b)
            tot = count(lambda blk, c: blk >= cand)
            return jnp.where(tot >= topk, cand, thr)

        thr = lax.fori_loop(0, 32, bit_body, jnp.full((rg, 1), INT_MIN, jnp.int32))
        n_gt = count(lambda blk, c: blk > thr)
        n_eq = count(lambda blk, c: blk == thr)
        need = topk - n_gt
        t_ref[pl.ds(r0, rg), :] = thr
        j_ref[pl.ds(r0, rg), :] = jnp.full((rg, 1), 2 ** 30, jnp.int32)
        excess = jnp.max(jnp.where(thr > INT_MIN, n_eq - need, 0))

        @pl.when(excess > 0)
        def _():
            lane_r = lax.broadcasted_iota(jnp.int32, (rg, tq), 1)

            def idx_body(b, pos):
                cand = pos + lax.shift_left(jnp.int32(1), idx_bits - 1 - b)
                tot = count(lambda blk, c: (blk == thr) & (lane_r + c * tq < cand))
                return jnp.where(tot < need, cand, pos)

            j_ref[pl.ds(r0, rg), :] = lax.fori_loop(
                0, idx_bits, idx_body, jnp.zeros((rg, 1), jnp.int32))
        return carry

    lax.fori_loop(0, tq // rg, select_rows, 0)

    m_ref[...] = jnp.full(m_ref.shape, NEG_BIG, F32)
    l_ref[...] = jnp.zeros(l_ref.shape, F32)
    acc_ref[...] = jnp.zeros(acc_ref.shape, F32)
    thr = t_ref[...]
    jlim = j_ref[...]
    q = qcat_ref[0].reshape(n_heads * tq, QCAT)

    def att_chunk(c, carry):
        key = sc_ref[c]
        col = lane + c * tq
        sel = (key > thr) | ((key == thr) & (col <= jlim))
        sel = sel & (lane + (c - i) * tq <= sub)
        kc = kcat_ref[0, pl.ds(pl.multiple_of(c * tq, tq), tq), :]
        lg = (_dot_nt(q, kc) * scale).reshape(n_heads, tq, tq)
        lg = jnp.where(sel[None], lg, NEG_BIG)
        m_old = m_ref[...]
        m_new = jnp.maximum(m_old, jnp.max(lg, axis=-1, keepdims=True))
        p = jnp.exp(lg - m_new)
        alpha = jnp.exp(m_old - m_new)
        l_ref[...] = alpha * l_ref[...] + jnp.sum(p, axis=-1, keepdims=True)
        pv = _dot(p.reshape(n_heads * tq, tq).astype(BF16), kc[:, :KV_RANK])
        acc_ref[...] = alpha * acc_ref[...] + pv.reshape(n_heads, tq, KV_RANK)
        m_ref[...] = m_new
        return carry

    lax.fori_loop(0, nch, att_chunk, 0)
    o_lat = (acc_ref[...] / l_ref[...]).astype(BF16)
    for h in range(n_heads):
        o_ref[0, :, h * HEAD_DIM:(h + 1) * HEAD_DIM] = _dot(o_lat[h], wuv_ref[h]).astype(BF16)


def _dsa(qi, wi, qcat, ki, kcat, wuv, b, s, tq=256, rg=64):
    tq = min(tq, s)
    rg = min(rg, tq)
    topk = min(TOPK_MAX, s // 4)
    wq = N_IDX_HEADS * LANES
    return pl.pallas_call(
        functools.partial(_dsa_kernel, tq=tq, rg=rg, topk=topk, idx_bits=s.bit_length(),
                          scale=HEAD_DIM ** -0.5),
        grid=(b, s // tq),
        in_specs=[pl.BlockSpec((1, tq, wq), lambda bb, i: (bb, i, 0)),
                  pl.BlockSpec((1, tq, LANES), lambda bb, i: (bb, i, 0)),
                  pl.BlockSpec((1, N_HEADS_A, tq, QCAT), lambda bb, i: (bb, 0, i, 0)),
                  pl.BlockSpec((1, s, LANES), lambda bb, i: (bb, 0, 0)),
                  pl.BlockSpec((1, s, QCAT), lambda bb, i: (bb, 0, 0)),
                  pl.BlockSpec(wuv.shape, lambda bb, i: (0, 0, 0))],
        out_specs=pl.BlockSpec((1, tq, N_HEADS_A * HEAD_DIM), lambda bb, i: (bb, i, 0)),
        out_shape=jax.ShapeDtypeStruct((b, s, N_HEADS_A * HEAD_DIM), BF16),
        scratch_shapes=[pltpu.VMEM((s // tq, tq, tq), jnp.int32),
                        pltpu.VMEM((N_IDX_HEADS, tq, LANES), F32),
                        pltpu.VMEM((tq, 1), jnp.int32),
                        pltpu.VMEM((tq, 1), jnp.int32),
                        pltpu.VMEM((N_HEADS_A, tq, 1), F32),
                        pltpu.VMEM((N_HEADS_A, tq, 1), F32),
                        pltpu.VMEM((N_HEADS_A, tq, KV_RANK), F32)],
        compiler_params=_cp("parallel", "parallel"),
    )(qi.reshape(b, s, wq), wi.reshape(b, s, LANES), qcat,
      ki.reshape(b, s, LANES), kcat.reshape(b, s, QCAT), wuv)


def _fox_kernel(q_ref, k_ref, v_ref, ck_ref, o_ref, m_ref, l_ref, acc_ref, *, tq, scale):
    i = pl.program_id(2)
    q = q_ref[0]
    m_ref[...] = jnp.full(m_ref.shape, NEG_BIG, F32)
    l_ref[...] = jnp.zeros(l_ref.shape, F32)
    acc_ref[...] = jnp.zeros(acc_ref.shape, F32)
    causal = (lax.broadcasted_iota(jnp.int32, (tq, tq), 1)
              <= lax.broadcasted_iota(jnp.int32, (tq, tq), 0))

    def step(c, diag):
        rows = pl.ds(pl.multiple_of(c * tq, tq), tq)
        s = _dot_nt(q, k_ref[0, rows, :]) * scale - ck_ref[0, 0, c]
        if diag:
            s = jnp.where(causal, s, NEG_BIG)
        m_old = m_ref[...]
        m_new = jnp.maximum(m_old, jnp.max(s, axis=-1, keepdims=True))
        p = jnp.exp(s - m_new)
        alpha = jnp.exp(m_old - m_new)
        l_ref[...] = alpha * l_ref[...] + jnp.sum(p, axis=-1, keepdims=True)
        acc_ref[...] = alpha * acc_ref[...] + _dot(p.astype(BF16), v_ref[0, rows, :])
        m_ref[...] = m_new

    def body(c, carry):
        step(c, False)
        return carry

    lax.fori_loop(0, i, body, 0)
    step(i, True)
    o_ref[0] = (acc_ref[...] / l_ref[...]).astype(BF16)


def _fox(qkv, cum_k, b, s, col0, tq=256):
    tq = min(tq, s)
    h = N_HEADS_B
    return pl.pallas_call(
        functools.partial(_fox_kernel, tq=tq, scale=HEAD_DIM ** -0.5),
        grid=(b, h, s // tq),
        in_specs=[pl.BlockSpec((1, tq, HEAD_DIM), lambda bb, hh, i: (bb, i, col0 + hh)),
                  pl.BlockSpec((1, s, HEAD_DIM), lambda bb, hh, i: (bb, 0, col0 + h + hh)),
                  pl.BlockSpec((1, s, HEAD_DIM), lambda bb, hh, i: (bb, 0, col0 + 2 * h + hh)),
                  pl.BlockSpec((1, 1, s // tq, 1, tq), lambda bb, hh, i: (bb, hh, 0, 0, 0))],
        out_specs=pl.BlockSpec((1, tq, HEAD_DIM), lambda bb, hh, i: (bb, i, hh)),
        out_shape=jax.ShapeDtypeStruct((b, s, h * HEAD_DIM), BF16),
        scratch_shapes=[pltpu.VMEM((tq, 1), F32), pltpu.VMEM((tq, 1), F32),
                        pltpu.VMEM((tq, HEAD_DIM), F32)],
        compiler_params=_cp("parallel", "parallel", "parallel"),
    )(qkv, qkv, qkv, cum_k.reshape(b, h, s // tq, 1, tq))


def _sb_kernel(q_ref, k_ref, v_ref, o_ref, r_ref, acc_ref, *, tq, scale):
    i = pl.program_id(2)
    q = q_ref[0]
    r_ref[...] = jnp.zeros(r_ref.shape, F32)
    acc_ref[...] = jnp.zeros(acc_ref.shape, F32)
    sub = lax.broadcasted_iota(jnp.int32, (tq, tq), 0)
    lane = lax.broadcasted_iota(jnp.int32, (tq, tq), 1)
    strict = lane < sub
    later = (sub > lane).astype(BF16)

    def step(c, diag):
        rows = pl.ds(pl.multiple_of(c * tq, tq), tq)
        z = _dot_nt(q, k_ref[0, rows, :]) * scale
        sp = jnp.maximum(z, 0.0) + jnp.log1p(jnp.exp(-jnp.abs(z)))
        lk = -sp
        if diag:
            lk = jnp.where(strict, lk, 0.0)
        hi = lk.astype(BF16)
        lo = (lk - hi.astype(F32)).astype(BF16)
        tail = _dot(hi, later) + _dot(lo, later)
        w = jnp.exp(z - sp + tail + r_ref[...])
        if diag:
            w = jnp.where(strict, w, 0.0)
        acc_ref[...] += _dot(w.astype(BF16), v_ref[0, rows, :])
        r_ref[...] += jnp.sum(lk, axis=-1, keepdims=True)

    step(i, True)

    def body(jj, carry):
        step(i - 1 - jj, False)
        return carry

    lax.fori_loop(0, i, body, 0)
    o_ref[0] = acc_ref[...].astype(BF16)


def _sb(qkv, b, s, col0, tq=256):
    tq = min(tq, s)
    h = N_HEADS_C
    return pl.pallas_call(
        functools.partial(_sb_kernel, tq=tq, scale=HEAD_DIM ** -0.5),
        grid=(b, h, s // tq),
        in_specs=[pl.BlockSpec((1, tq, HEAD_DIM), lambda bb, hh, i: (bb, i, col0 + hh)),
                  pl.BlockSpec((1, s, HEAD_DIM), lambda bb, hh, i: (bb, 0, col0 + h + hh)),
                  pl.BlockSpec((1, s, HEAD_DIM), lambda bb, hh, i: (bb, 0, col0 + 2 * h + hh))],
        out_specs=pl.BlockSpec((1, tq, HEAD_DIM), lambda bb, hh, i: (bb, i, hh)),
        out_shape=jax.ShapeDtypeStruct((b, s, h * HEAD_DIM), BF16),
        scratch_shapes=[pltpu.VMEM((tq, 1), F32), pltpu.VMEM((tq, HEAD_DIM), F32)],
        compiler_params=_cp("parallel", "parallel", "parallel"),
    )(qkv, qkv, qkv)


def _merge_kernel(oa_ref, ob_ref, oc_ref, wa_ref, wb_ref, wc_ref, ga_ref, gb_ref, gc_ref, o_ref):
    y = (ga_ref[...].astype(F32) * _dot(oa_ref[...], wa_ref[...])
         + gb_ref[...].astype(F32) * _dot(ob_ref[...], wb_ref[...])
         + gc_ref[...].astype(F32) * _dot(oc_ref[...], wc_ref[...]))
    o_ref[...] = y.astype(o_ref.dtype)


def _merge(oa, ob, oc, wa, wb, wc, gates, tm=512, tn=512):
    n, k = oa.shape
    d = wa.shape[1]
    tm, tn = min(tm, n), min(tn, d)
    nb = d // tn
    lhs = pl.BlockSpec((tm, k), lambda i, j: (i, 0))
    rhs = pl.BlockSpec((k, tn), lambda i, j: (0, j))
    gate = lambda br: pl.BlockSpec((tm, tn), lambda i, j: (i, br * nb + j))
    return pl.pallas_call(
        _merge_kernel,
        grid=(n // tm, nb),
        in_specs=[lhs, lhs, lhs, rhs, rhs, rhs, gate(0), gate(1), gate(2)],
        out_specs=pl.BlockSpec((tm, tn), lambda i, j: (i, j)),
        out_shape=jax.ShapeDtypeStruct((n, d), BF16),
        compiler_params=_cp("parallel", "parallel"),
    )(oa, ob, oc, wa, wb, wc, gates, gates, gates)


def _outln_kernel(m_ref, w_ref, x_ref, g_ref, b_ref, of_ref, ob_ref, *, alpha):
    h = alpha * x_ref[...] + _dot(m_ref[...], w_ref[...])
    y = _layer_norm(h, g_ref[...], b_ref[...])
    of_ref[...] = y
    ob_ref[...] = y.astype(BF16)


def _outln(merged, w_out, x, g, bias, alpha, tm=256):
    n, d = x.shape
    tm = min(tm, n)
    row = pl.BlockSpec((tm, d), lambda i: (i, 0))
    vec = pl.BlockSpec((1, d), lambda i: (0, 0))
    return pl.pallas_call(
        functools.partial(_outln_kernel, alpha=alpha),
        grid=(n // tm,),
        in_specs=[row, pl.BlockSpec((d, d), lambda i: (0, 0)), row, vec, vec],
        out_specs=[row, row],
        out_shape=[jax.ShapeDtypeStruct((n, d), F32), jax.ShapeDtypeStruct((n, d), BF16)],
        compiler_params=_cp("parallel"),
    )(merged, w_out, x, g.reshape(1, d), bias.reshape(1, d))


def _router_kernel(x_ref, w_ref, b_ref, cw_ref):
    xh, xm, _ = _split3(x_ref[...])
    wh, wm, _ = _split3(w_ref[...])
    logits = _dot(xh, wh) + _dot(xh, wm) + _dot(xm, wh) + b_ref[...]
    n_exp = logits.shape[-1]
    lane = lax.broadcasted_iota(jnp.int32, logits.shape, 1)
    work = logits
    picks, vals = [], []
    for _ in range(TOP_K_EXPERTS):
        top = jnp.max(work, axis=-1, keepdims=True)
        idx = jnp.min(jnp.where(work == top, lane, n_exp), axis=-1, keepdims=True)
        pick = lane == idx
        picks.append(pick)
        vals.append(top)
        work = jnp.where(pick, -jnp.inf, work)
    exps = [jnp.exp(v - vals[0]) for v in vals]
    denom = exps[0]
    for e in exps[1:]:
        denom = denom + e
    cw = jnp.zeros(logits.shape, F32)
    for pick, e in zip(picks, exps):
        cw = jnp.where(pick, e / denom, cw)
    cw_ref[...] = cw


def _router(x, w_router, b_router, tm=512):
    n, d = x.shape
    e = w_router.shape[1]
    tm = min(tm, n)
    return pl.pallas_call(
        _router_kernel,
        grid=(n // tm,),
        in_specs=[pl.BlockSpec((tm, d), lambda i: (i, 0)),
                  pl.BlockSpec((d, e), lambda i: (0, 0)),
                  pl.BlockSpec((1, e), lambda i: (0, 0))],
        out_specs=pl.BlockSpec((tm, e), lambda i: (i, 0)),
        out_shape=jax.ShapeDtypeStruct((n, e), F32),
        compiler_params=_cp("parallel"),
    )(x, w_router.astype(F32), b_router.reshape(1, e).astype(F32))


def _moe_kernel(x_ref, cw_ref, wg_ref, bg_ref, wu_ref, bu_ref, wd_ref, bd_ref, o_ref):
    e = pl.program_id(1)

    @pl.when(e == 0)
    def _():
        o_ref[...] = jnp.zeros_like(o_ref)

    x = x_ref[...]
    g = jnp.minimum(_dot(x, wg_ref[0]) + bg_ref[0], SWIGLU_LIMIT)
    u = jnp.clip(_dot(x, wu_ref[0]) + bu_ref[0], -SWIGLU_LIMIT, SWIGLU_LIMIT)
    hid = (u + 1.0) * (g * jax.nn.sigmoid(SWIGLU_ALPHA * g))
    y = _dot(hid.astype(BF16), wd_ref[0]) + bd_ref[0]
    cw = cw_ref[...]
    lane = lax.broadcasted_iota(jnp.int32, cw.shape, 1)
    c = jnp.sum(jnp.where(lane == e, cw, 0.0), axis=-1, keepdims=True)
    o_ref[...] += c * y


def _moe(x, cw, wg, bg, wu, bu, wd, bd, tm=512):
    n, d = x.shape
    e, _, f = wg.shape
    tm = min(tm, n)
    row = pl.BlockSpec((tm, d), lambda i, j: (i, 0))
    return pl.pallas_call(
        _moe_kernel,
        grid=(n // tm, e),
        in_specs=[row, pl.BlockSpec((tm, e), lambda i, j: (i, 0)),
                  pl.BlockSpec((1, d, f), lambda i, j: (j, 0, 0)),
                  pl.BlockSpec((1, 1, f), lambda i, j: (j, 0, 0)),
                  pl.BlockSpec((1, d, f), lambda i, j: (j, 0, 0)),
                  pl.BlockSpec((1, 1, f), lambda i, j: (j, 0, 0)),
                  pl.BlockSpec((1, f, d), lambda i, j: (j, 0, 0)),
                  pl.BlockSpec((1, 1, d), lambda i, j: (j, 0, 0))],
        out_specs=row,
        out_shape=jax.ShapeDtypeStruct((n, d), F32),
        compiler_params=_cp("parallel", "arbitrary"),
    )(x, cw, wg, bg.reshape(e, 1, f), wu, bu.reshape(e, 1, f), wd, bd.reshape(e, 1, d))


def _ple_kernel(x_ref, y_ref, p_ref, wg_ref, wp_ref, g_ref, b_ref, of_ref, ob_ref, *, alpha):
    h = alpha * x_ref[...] + y_ref[...]
    gate = jax.nn.sigmoid(_dot(h.astype(BF16), wg_ref[...]))
    z = h + gate * _dot(p_ref[...], wp_ref[...])
    y = _layer_norm(z, g_ref[...], b_ref[...])
    of_ref[...] = y
    ob_ref[...] = y.astype(BF16)


def _ple(x, y, p, w_gate, w_ple, g, bias, alpha, tm=256):
    n, d = x.shape
    pd = p.shape[1]
    tm = min(tm, n)
    row = pl.BlockSpec((tm, d), lambda i: (i, 0))
    vec = pl.BlockSpec((1, d), lambda i: (0, 0))
    return pl.pallas_call(
        functools.partial(_ple_kernel, alpha=alpha),
        grid=(n // tm,),
        in_specs=[row, row, pl.BlockSpec((tm, pd), lambda i: (i, 0)),
                  pl.BlockSpec((d, d), lambda i: (0, 0)),
                  pl.BlockSpec((pd, d), lambda i: (0, 0)), vec, vec],
        out_specs=[row, row],
        out_shape=[jax.ShapeDtypeStruct((n, d), F32), jax.ShapeDtypeStruct((n, d), BF16)],
        compiler_params=_cp("parallel"),
    )(x, y, p, w_gate, w_ple, g.reshape(1, d), bias.reshape(1, d))


def _rope_tables(s):
    pos = jnp.arange(s, dtype=F32)[:, None]
    lane = jnp.arange(LANES)

    def table(rot_dim, period, active_lanes):
        inv = ROPE_THETA ** (-jnp.arange(0, rot_dim, 2, dtype=F32) / rot_dim)
        ang = pos * inv[None, :]
        j = lane % period
        idx = j % (rot_dim // 2)
        rot = (j < rot_dim) & (lane < active_lanes)
        cos = jnp.where(rot[None], jnp.cos(ang)[:, idx], 1.0)
        sin = jnp.where(rot[None], jnp.sin(ang)[:, idx], 0.0)
        return cos, sin

    cos_r, sin_r = table(ROPE_DIM, HEAD_DIM, LANES)
    cos_r = jnp.where((lane < ROPE_DIM)[None], cos_r, 0.0)
    cos_i, sin_i = table(IDX_ROPE_DIM, IDX_DIM, LANES)
    cos_ik, sin_ik = table(IDX_ROPE_DIM, IDX_DIM, IDX_DIM)
    return dict(cos_r=cos_r, sin_r=sin_r, cos_i=cos_i, sin_i=sin_i, cos_ik=cos_ik, sin_ik=sin_ik)


def _in_proj_weights(w, d):
    wa, wb, wc = N_HEADS_A * HEAD_DIM, N_HEADS_B * HEAD_DIM, N_HEADS_C * HEAD_DIM
    widths = (wa, KV_RANK, ROPE_DIM, N_IDX_HEADS * IDX_DIM, IDX_DIM, N_IDX_HEADS,
              wb, wb, wb, N_HEADS_B, wc, wc, wc, N_BRANCH * d)
    assert sum(widths) == w.shape[1]
    parts, start = [], 0
    for width in widths:
        parts.append(w[:, start:start + width])
        start += width
    (q_a, ckv, k_rope, q_idx, k_idx, w_idx, q_b, k_b, v_b, f_b, q_c, k_c, v_c, gates) = parts
    pad = lambda t: jnp.pad(t, ((0, 0), (0, LANES - t.shape[1])))
    w_f32 = jnp.concatenate([q_a, q_idx, ckv, pad(k_rope), pad(k_idx), pad(w_idx), pad(f_b)], axis=1)
    w_qkv = jnp.concatenate([q_b, k_b, v_b, q_c, k_c, v_c], axis=1)
    return w_f32.astype(BF16), w_qkv.astype(BF16), gates.astype(BF16)


def _query_weights(w_uk):
    r, h, nope = w_uk.shape
    uk = jnp.transpose(w_uk, (1, 2, 0))
    top = jnp.zeros((h, HEAD_DIM, QCAT), F32).at[:, ROPE_DIM:, :r].set(uk)
    eye = jnp.zeros((HEAD_DIM, QCAT), F32).at[jnp.arange(ROPE_DIM), r + jnp.arange(ROPE_DIM)].set(1.0)
    bot = jnp.broadcast_to(eye, (h, HEAD_DIM, QCAT))
    return jnp.concatenate([top, bot], axis=1).astype(BF16)


def _idx_placement():
    src = jnp.arange(N_IDX_HEADS * IDX_DIM)
    dst = (src // IDX_DIM) * LANES + src % IDX_DIM
    return jnp.zeros((N_IDX_HEADS * IDX_DIM, N_IDX_HEADS * LANES), BF16).at[src, dst].set(1.0)


def kernel(x, p, w_in, b_forget, g_ckv, w_uk, w_uv, w_br_a, w_br_b, w_br_c, w_out, ln1_g, ln1_b,
           w_router, b_router, w_exp_gate, b_exp_gate, w_exp_up, b_exp_up, w_exp_down, b_exp_down,
           w_ple_gate, w_ple, ln2_g, ln2_b):
    b, s, d = x.shape
    depth = w_in.shape[0]
    n = b * s
    alpha = (2 * depth) ** 0.25
    tabs = _rope_tables(s)
    place = _idx_placement()
    hb = N_HEADS_B

    xf = x.reshape(n, d).astype(F32)
    xb = xf.astype(BF16)
    for i in range(depth):
        w_f32, w_qkv, w_gates = _in_proj_weights(w_in[i], d)
        hf = _matmul(xb, w_f32, F32)
        qkv = _matmul(xb, w_qkv, BF16).reshape(b, s, -1)
        gates = _matmul(xb, w_gates, BF16, act="sigmoid")

        qcat, qi = _prep_q(hf, tabs, _query_weights(w_uk[i]), place, b, s)
        kcat, ki, wi, cum = _prep_k(hf, tabs, g_ckv[i], b_forget[i], b, s)
        wuv = jnp.transpose(w_uv[i], (1, 0, 2)).astype(BF16)
        o_a = _dsa(qi, wi, qcat, ki, kcat, wuv, b, s)
        cum_k = jnp.transpose(cum.reshape(b, s, LANES)[:, :, :hb], (0, 2, 1))
        o_b = _fox(qkv, cum_k, b, s, 0)
        o_c = _sb(qkv, b, s, 3 * hb)

        merged = _merge(o_a.reshape(n, -1), o_b.reshape(n, -1), o_c.reshape(n, -1),
                        w_br_a[i].astype(BF16), w_br_b[i].astype(BF16), w_br_c[i].astype(BF16), gates)
        x1f, x1b = _outln(merged, w_out[i].astype(BF16), xf, ln1_g[i], ln1_b[i], alpha)

        cw = _router(x1f, w_router[i], b_router[i])
        y = _moe(x1b, cw, w_exp_gate[i].astype(BF16), b_exp_gate[i], w_exp_up[i].astype(BF16),
                 b_exp_up[i], w_exp_down[i].astype(BF16), b_exp_down[i])
        xf, xb = _ple(x1f, y, p[i].reshape(n, -1).astype(BF16), w_ple_gate[i].astype(BF16),
                      w_ple[i].astype(BF16), ln2_g[i], ln2_b[i], alpha)
    return xf.reshape(b, s, d).astype(x.dtype)
```
